```python
import jax, jax.numpy as jnp
from jax import lax
import numpy as np

D_MODEL = 1024
BATCH = 2
SEQ = 8192
DEPTH = 2

W_A = D_MODEL
K_A = 3
W_B = D_MODEL
K_B = 31
W_C = D_MODEL
POOL_WINDOWS = (2, 4, 8, 16)
N_POOL_GROUPS = len(POOL_WINDOWS)
GC = W_C // N_POOL_GROUPS
N_BRANCH = 3
D_FF = 4 * D_MODEL
EPS = 1e-6
COLS_A = 3 * W_A
COLS_B = 2 * W_B
COLS_C = W_C
COLS_G = N_BRANCH * D_MODEL
P_IN = COLS_A + COLS_B + COLS_C + COLS_G
SPLITS = (W_A, 2 * W_A, 3 * W_A, 3 * W_A + W_B, COLS_A + COLS_B, COLS_A + COLS_B + COLS_C)

kernel_name = "hybrid_conv_pool_gated_block"


def rmsnorm(x, g):
    x32 = x.astype(jnp.float32)
    y = x32 * lax.rsqrt(jnp.mean(x32 * x32, axis=-1, keepdims=True) + EPS)
    return (y * g.astype(jnp.float32)).astype(x.dtype)


def layernorm(x, g, b):
    x32 = x.astype(jnp.float32)
    mu = jnp.mean(x32, axis=-1, keepdims=True)
    xc = x32 - mu
    var = jnp.mean(xc * xc, axis=-1, keepdims=True)
    y = xc * lax.rsqrt(var + EPS) * g.astype(jnp.float32) + b.astype(jnp.float32)
    return y.astype(x.dtype)


def causal_depthwise_conv(u, w):
    k, c = w.shape
    return lax.conv_general_dilated(
        u, w[:, None, :].astype(u.dtype), window_strides=(1,), padding=[(k - 1, 0)],
        dimension_numbers=("NWC", "WIO", "NWC"), feature_group_count=c)


def short_conv_mixer(b_gate, c_gate, xh, conv_w, w_out):
    z = causal_depthwise_conv(c_gate * xh, conv_w)
    return (b_gate * z) @ w_out


def conformer_conv_mixer(val, gate, conv_w, conv_b, ln_g, ln_b, w_out, b_out):
    u = val * jax.nn.sigmoid(gate)
    u = causal_depthwise_conv(u, conv_w) + conv_b
    u = jax.nn.silu(layernorm(u, ln_g, ln_b))
    return u @ w_out + b_out


def pool_mixer(u, w_pool, scale):
    seq = u.shape[1]
    u32 = u.astype(jnp.float32)
    cs = jnp.cumsum(u32, axis=1)
    t = jnp.arange(seq)
    parts = []
    for g, w in enumerate(POOL_WINDOWS):
        sl = slice(g * GC, (g + 1) * GC)
        c = cs[..., sl]
        prev = jnp.pad(c, ((0, 0), (w, 0), (0, 0)))[:, :seq]
        cnt = jnp.minimum(t + 1, w).astype(jnp.float32)[None, :, None]
        parts.append((c - prev) / cnt - u32[..., sl])
    p = jnp.stack(parts, axis=2).astype(u.dtype)
    y = jnp.einsum('bsgc,gcd->bsgd', p, w_pool)
    return y.reshape(u.shape[0], seq, W_C) * scale


def setup_inputs(seed: int = 0) -> dict:
    key = jax.random.key(seed)
    ks = jax.random.split(key, 20)
    L, D = DEPTH, D_MODEL
    f32 = jnp.float32

    def nrm(k, shape, fan_in):
        return jax.random.normal(k, shape, f32) * (fan_in ** -0.5)

    def gain(k, shape):
        return 1.0 + 0.02 * jax.random.normal(k, shape, f32)

    return {
        "x": jax.random.normal(ks[0], (BATCH, SEQ, D), f32),
        "g_mix": gain(ks[1], (L, D)),
        "w_in": nrm(ks[2], (L, D, P_IN), D),
        "b_in": 0.02 * jax.random.normal(ks[3], (L, P_IN), f32),
        "conv_a": nrm(ks[4], (L, K_A, W_A), K_A),
        "w_out_a": nrm(ks[5], (L, W_A, D), W_A),
        "conv_b": nrm(ks[6], (L, K_B, W_B), K_B),
        "conv_b_bias": 0.02 * jax.random.normal(ks[7], (L, W_B), f32),
        "ln_b_g": gain(ks[8], (L, W_B)),
        "ln_b_b": 0.02 * jax.random.normal(ks[9], (L, W_B), f32),
        "w_out_b": nrm(ks[10], (L, W_B, D), W_B),
        "b_out_b": 0.02 * jax.random.normal(ks[11], (L, D), f32),
        "w_pool": nrm(ks[12], (L, N_POOL_GROUPS, GC, GC), GC),
        "pool_scale": gain(ks[13], (L, W_C)),
        "w_o": nrm(ks[14], (L, D, D), D),
        "g_mlp": gain(ks[15], (L, D)),
        "w_mlp1": nrm(ks[16], (L, D, D_FF), D),
        "w_mlp2": nrm(ks[17], (L, D_FF, D), D_FF),
        "g_final": gain(ks[18], (D,)),
    }


def reference(x, g_mix, w_in, b_in, conv_a, w_out_a, conv_b, conv_b_bias, ln_b_g, ln_b_b,
              w_out_b, b_out_b, w_pool, pool_scale, w_o, g_mlp, w_mlp1, w_mlp2, g_final):
    bsz, seq, d = x.shape
    for l in range(DEPTH):
        h = rmsnorm(x, g_mix[l])
        proj = h @ w_in[l] + b_in[l]
        a_b, a_c, a_x, b_val, b_gate, c_in, gates = jnp.split(proj, SPLITS, axis=-1)
        y_a = short_conv_mixer(a_b, a_c, a_x, conv_a[l], w_out_a[l])
        y_b = conformer_conv_mixer(b_val, b_gate, conv_b[l], conv_b_bias[l], ln_b_g[l],
                                   ln_b_b[l], w_out_b[l], b_out_b[l])
        y_c = pool_mixer(c_in, w_pool[l], pool_scale[l])
        g = jax.nn.sigmoid(gates).reshape(bsz, seq, N_BRANCH, d)
        merged = g[:, :, 0] * y_a + g[:, :, 1] * y_b + g[:, :, 2] * y_c
        x = x + merged @ w_o[l]
        h = rmsnorm(x, g_mlp[l])
        x = x + jnp.square(jax.nn.relu(h @ w_mlp1[l])) @ w_mlp2[l]
    return rmsnorm(x, g_final)
```

```python
import functools

import jax
import jax.numpy as jnp
from jax import lax
from jax.experimental import pallas as pl
from jax.experimental.pallas import tpu as pltpu

EPS = 1e-6
K_A = 3
K_B = 31
POOL_WINDOWS = (2, 4, 8, 16)
N_BRANCH = 3

SUBLANES = 8
LANES = 128
VMEM_LIMIT_BYTES = 60 * 1024 * 1024

HIST_A = SUBLANES
HIST_B = 4 * SUBLANES
HIST_C = 2 * SUBLANES
CONV_ROWS = 8 * SUBLANES

V_G_MIX, V_CONV_B_BIAS, V_LN_G, V_LN_B, V_B_OUT_B, V_POOL_SCALE = range(6)
N_VEC = 8


def _rmsnorm(x, g):
    return x * lax.rsqrt(jnp.mean(x * x, axis=-1, keepdims=True) + EPS) * g


def _dot(a, b):
    return jnp.dot(a, b, preferred_element_type=jnp.float32)


def _mixer_kernel(x_ref, vec_ref, w_in_ref, b_in_ref, conv_a_ref, w_out_a_ref, conv_b_ref,
                  w_out_b_ref, w_pool_ref, w_o_ref, o_ref,
                  cx_buf, u_buf, c_buf, v_buf, *, tiles_per_seq):
    tm, d = x_ref.shape
    ncb = d // LANES
    gc = d // len(POOL_WINDOWS)
    seq_tile = lax.rem(pl.program_id(0), tiles_per_seq)

    @pl.when(seq_tile == 0)
    def _():
        cx_buf[:, 0:HIST_A, :] = jnp.zeros((ncb, HIST_A, LANES), jnp.float32)
        u_buf[:, 0:HIST_B, :] = jnp.zeros((ncb, HIST_B, LANES), jnp.float32)
        c_buf[:, 0:HIST_C, :] = jnp.zeros((ncb, HIST_C, LANES), jnp.float32)

    x = x_ref[...]
    h = _rmsnorm(x, vec_ref[V_G_MIX:V_G_MIX + 1, :]).astype(jnp.bfloat16)

    def proj(col):
        lo = col * d
        return _dot(h, w_in_ref[:, lo:lo + d]) + b_in_ref[:, lo:lo + d]

    def gate(branch):
        return jax.nn.sigmoid(proj(6 + branch))

    cx = proj(1) * proj(2)
    for cb in range(ncb):
        cx_buf[cb, HIST_A:HIST_A + tm, :] = cx[:, cb * LANES:(cb + 1) * LANES]
    for cb in range(ncb):
        cols = slice(cb * LANES, (cb + 1) * LANES)
        z = None
        for k in range(K_A):
            start = HIST_A - (K_A - 1) + k
            term = conv_a_ref[k:k + 1, cols] * cx_buf[cb, start:start + tm, :]
            z = term if z is None else z + term
        v_buf[:, cols] = z
        cx_buf[cb, 0:HIST_A, :] = cx_buf[cb, tm:tm + HIST_A, :]
    y_a = _dot((proj(0) * v_buf[...]).astype(jnp.bfloat16), w_out_a_ref[...])
    merged = gate(0) * y_a

    u = proj(3) * jax.nn.sigmoid(proj(4))
    for cb in range(ncb):
        u_buf[cb, HIST_B:HIST_B + tm, :] = u[:, cb * LANES:(cb + 1) * LANES]

    def conv_b_block(cb, carry):
        cols = pl.ds(pl.multiple_of(cb * LANES, LANES), LANES)
        for r0 in range(0, tm, CONV_ROWS):
            acc = [None] * (CONV_ROWS // SUBLANES)
            for k in range(K_B):
                wk = conv_b_ref[k, :, cols]
                for j in range(CONV_ROWS // SUBLANES):
                    start = HIST_B - (K_B - 1) + k + r0 + j * SUBLANES
                    term = wk * u_buf[cb, start:start + SUBLANES, :]
                    acc[j] = term if acc[j] is None else acc[j] + term
            for j in range(CONV_ROWS // SUBLANES):
                v_buf[r0 + j * SUBLANES:r0 + (j + 1) * SUBLANES, cols] = acc[j]
        u_buf[cb, 0:HIST_B, :] = u_buf[cb, tm:tm + HIST_B, :]
        return carry

    lax.fori_loop(0, ncb, conv_b_block, 0)
    v = v_buf[...] + vec_ref[V_CONV_B_BIAS:V_CONV_B_BIAS + 1, :]
    mu = jnp.mean(v, axis=-1, keepdims=True)
    vc = v - mu
    var = jnp.mean(vc * vc, axis=-1, keepdims=True)
    ln = (vc * lax.rsqrt(var + EPS) * vec_ref[V_LN_G:V_LN_G + 1, :]
          + vec_ref[V_LN_B:V_LN_B + 1, :])
    s = ln * jax.nn.sigmoid(ln)
    y_b = _dot(s.astype(jnp.bfloat16), w_out_b_ref[...]) + vec_ref[V_B_OUT_B:V_B_OUT_B + 1, :]
    merged = merged + gate(1) * y_b

    c_in = proj(5)
    for cb in range(ncb):
        c_buf[cb, HIST_C:HIST_C + tm, :] = c_in[:, cb * LANES:(cb + 1) * LANES]
    t_pos = seq_tile * tm + lax.broadcasted_iota(jnp.int32, (tm, 1), 0)
    for cb in range(ncb):
        cols = slice(cb * LANES, (cb + 1) * LANES)
        w = POOL_WINDOWS[(cb * LANES) // gc]
        win = None
        for j in range(w):
            term = c_buf[cb, HIST_C - j:HIST_C - j + tm, :]
            win = term if win is None else win + term
        cnt = jnp.minimum(t_pos + 1, w).astype(jnp.float32)
        v_buf[:, cols] = win / cnt - c_in[:, cols]
        c_buf[cb, 0:HIST_C, :] = c_buf[cb, tm:tm + HIST_C, :]
    p = v_buf[...].astype(jnp.bfloat16)
    y_c = jnp.concatenate(
        [_dot(p[:, g * gc:(g + 1) * gc], w_pool_ref[g]) for g in range(len(POOL_WINDOWS))],
        axis=-1) * vec_ref[V_POOL_SCALE:V_POOL_SCALE + 1, :]
    merged = merged + gate(2) * y_c

    o_ref[...] = x + _dot(merged.astype(jnp.bfloat16), w_o_ref[...])


def _mlp_kernel(x_ref, g_ref, w1_ref, w2_ref, gf_ref, o_ref, *, final_norm):
    x = x_ref[...]
    h = _rmsnorm(x, g_ref[...]).astype(jnp.bfloat16)
    a = jnp.square(jnp.maximum(_dot(h, w1_ref[...]), 0.0)).astype(jnp.bfloat16)
    y = x + _dot(a, w2_ref[...])
    if final_norm:
        y = _rmsnorm(y, gf_ref[...])
    o_ref[...] = y


def _resident(shape):
    return pl.BlockSpec(shape, lambda i: (0,) * len(shape), pipeline_mode=pl.Buffered(1))


def _mixer_call(x2, vecs, w_in, b_in, conv_a, w_out_a, conv_b8, w_out_b, w_pool, w_o, *, seq, tm):
    n, d = x2.shape
    ncb = d // LANES
    tile = pl.BlockSpec((tm, d), lambda i: (i, 0))
    operands = (vecs, w_in, b_in, conv_a, w_out_a, conv_b8, w_out_b, w_pool, w_o)
    return pl.pallas_call(
        functools.partial(_mixer_kernel, tiles_per_seq=seq // tm),
        grid=(n // tm,),
        in_specs=[tile] + [_resident(a.shape) for a in operands],
        out_specs=tile,
        out_shape=jax.ShapeDtypeStruct((n, d), jnp.float32),
        scratch_shapes=[
            pltpu.VMEM((ncb, HIST_A + tm, LANES), jnp.float32),
            pltpu.VMEM((ncb, HIST_B + tm, LANES), jnp.float32),
            pltpu.VMEM((ncb, HIST_C + tm, LANES), jnp.float32),
            pltpu.VMEM((tm, d), jnp.float32),
        ],
        compiler_params=pltpu.CompilerParams(
            dimension_semantics=("arbitrary",), vmem_limit_bytes=VMEM_LIMIT_BYTES),
        name="mixer",
    )(x2, *operands)


def _mlp_call(x2, g, w1, w2, g_final, *, final_norm, tm):
    n, d = x2.shape
    tile = pl.BlockSpec((tm, d), lambda i: (i, 0))
    operands = (g, w1, w2, g_final)
    return pl.pallas_call(
        functools.partial(_mlp_kernel, final_norm=final_norm),
        grid=(n // tm,),
        in_specs=[tile] + [_resident(a.shape) for a in operands],
        out_specs=tile,
        out_shape=jax.ShapeDtypeStruct((n, d), jnp.float32),
        compiler_params=pltpu.CompilerParams(
            dimension_semantics=("arbitrary",), vmem_limit_bytes=VMEM_LIMIT_BYTES),
        name="mlp",
    )(x2, *operands)


def kernel(x, g_mix, w_in, b_in, conv_a, w_out_a, conv_b, conv_b_bias, ln_b_g, ln_b_b, w_out_b,
           b_out_b, w_pool, pool_scale, w_o, g_mlp, w_mlp1, w_mlp2, g_final):
    bsz, seq, d = x.shape
    depth = w_in.shape[0]
    tm_mixer = 512
    tm_mlp = 512
    assert seq % tm_mixer == 0 and (bsz * seq) % tm_mlp == 0 and d % LANES == 0
    assert conv_a.shape[1] == K_A and conv_b.shape[1] == K_B
    bf16 = jnp.bfloat16
    x2 = x.reshape(bsz * seq, d)
    g_final2 = g_final.reshape(1, d)
    for l in range(depth):
        vecs = jnp.zeros((N_VEC, d), jnp.float32)
        vecs = vecs.at[V_G_MIX].set(g_mix[l]).at[V_CONV_B_BIAS].set(conv_b_bias[l])
        vecs = vecs.at[V_LN_G].set(ln_b_g[l]).at[V_LN_B].set(ln_b_b[l])
        vecs = vecs.at[V_B_OUT_B].set(b_out_b[l]).at[V_POOL_SCALE].set(pool_scale[l])
        conv_b8 = jnp.broadcast_to(conv_b[l][:, None, :], (K_B, SUBLANES, d))
        x2 = _mixer_call(
            x2, vecs, w_in[l].astype(bf16), b_in[l].reshape(1, -1), conv_a[l],
            w_out_a[l].astype(bf16), conv_b8, w_out_b[l].astype(bf16), w_pool[l].astype(bf16),
            w_o[l].astype(bf16), seq=seq, tm=tm_mixer)
        x2 = _mlp_call(
            x2, g_mlp[l].reshape(1, d), w_mlp1[l].astype(bf16), w_mlp2[l].astype(bf16), g_final2,
            final_norm=(l == depth - 1), tm=tm_mlp)
    return x2.reshape(bsz, seq, d)
```

```python
import functools

import jax
import jax.numpy as jnp
from jax import lax
from jax.experimental import pallas as pl
from jax.experimental.pallas import tpu as pltpu

EPS = 1e-6
K_A = 3
K_B = 31
POOL_WINDOWS = (2, 4, 8, 16)
N_BRANCH = 3

SUBLANES = 8
LANES = 128
VMEM_LIMIT_BYTES = 60 * 1024 * 1024
PACK_BLOCK_BYTES = 8 * 1024 * 1024

HIST_A = SUBLANES
HIST_B = 4 * SUBLANES
HIST_C = 2 * SUBLANES
CONV_CHAINS = 3
CONV_BLOCKS_PER_STEP = 4
MLP_HIDDEN_CHUNKS = 2

V_G_MIX, V_CONV_B_BIAS, V_LN_G, V_LN_B, V_B_OUT_B, V_POOL_SCALE = range(6)
N_VEC = 8


def _rmsnorm(x, g):
    return x * lax.rsqrt(jnp.mean(x * x, axis=-1, keepdims=True) + EPS) * g


def _pack_kernel(w_ref, o_ref):
    o_ref[...] = pltpu.bitcast(w_ref[...].astype(jnp.bfloat16), jnp.uint32)


def _pack_weight(w):
    l, k, n = w.shape
    bk = k
    while bk * n * 4 > PACK_BLOCK_BYTES and bk % (4 * SUBLANES) == 0:
        bk //= 2
    return pl.pallas_call(
        _pack_kernel,
        grid=(l, k // bk),
        in_specs=[pl.BlockSpec((None, bk, n), lambda a, b: (a, b, 0))],
        out_specs=pl.BlockSpec((None, bk // 2, n), lambda a, b: (a, b, 0)),
        out_shape=jax.ShapeDtypeStruct((l, k // 2, n), jnp.uint32),
        compiler_params=pltpu.CompilerParams(
            dimension_semantics=("arbitrary", "arbitrary"),
            vmem_limit_bytes=4 * PACK_BLOCK_BYTES),
        name="pack",
    )(w)


def _dot(a, w_packed):
    return jnp.dot(a, pltpu.bitcast(w_packed, jnp.bfloat16), preferred_element_type=jnp.float32)


def _mixer_kernel(x_ref, vec_ref, w_in_ref, conv_a_ref, w_out_a_ref, conv_b_ref,
                  w_out_b_ref, w_pool_ref, w_o_ref, b_in_ref, o_ref,
                  h_buf, cx_buf, u_buf, c_buf, v_buf, pg_buf, lhs_buf, *, tiles_per_seq):
    tm, d = x_ref.shape
    ncb = d // LANES
    gc = d // len(POOL_WINDOWS)
    n_steps = ncb // CONV_BLOCKS_PER_STEP
    pg_chunk = pg_buf.shape[1] // n_steps
    ya_chunk = CONV_BLOCKS_PER_STEP * LANES
    seq_tile = lax.rem(pl.program_id(0), tiles_per_seq)

    @pl.when(seq_tile == 0)
    def _():
        cx_buf[:, 0:HIST_A, :] = jnp.zeros((ncb, HIST_A, LANES), jnp.float32)
        u_buf[:, 0:HIST_B, :] = jnp.zeros((ncb, HIST_B, LANES), jnp.float32)
        c_buf[:, 0:HIST_C, :] = jnp.zeros((ncb, HIST_C, LANES), jnp.float32)

    h_buf[...] = _rmsnorm(x_ref[...], vec_ref[V_G_MIX:V_G_MIX + 1, :]).astype(jnp.bfloat16)

    def proj(col):
        lo = col * d
        return _dot(h_buf[...], w_in_ref[:, lo:lo + d]) + b_in_ref[:, lo:lo + d]

    def vec(row):
        return vec_ref[row:row + 1, :]

    def pg_group(group):
        return pg_buf[:, group * d:(group + 1) * d] + b_in_ref[:, (5 + group) * d:(6 + group) * d]

    def gate(branch):
        return jax.nn.sigmoid(pg_group(1 + branch))

    u = proj(3) * jax.nn.sigmoid(proj(4))
    for cb in range(ncb):
        u_buf[cb, HIST_B:HIST_B + tm, :] = u[:, cb * LANES:(cb + 1) * LANES]

    cx = proj(1) * proj(2)
    a_b = proj(0)
    for cb in range(ncb):
        cols = slice(cb * LANES, (cb + 1) * LANES)
        cx_buf[cb, HIST_A:HIST_A + tm, :] = cx[:, cols]
        z = None
        for k in range(K_A):
            start = HIST_A - (K_A - 1) + k
            term = conv_a_ref[k:k + 1, cols] * cx_buf[cb, start:start + tm, :]
            z = term if z is None else z + term
        lhs_buf[:, cols] = (a_b[:, cols] * z).astype(jnp.bfloat16)
        cx_buf[cb, 0:HIST_A, :] = cx_buf[cb, tm:tm + HIST_A, :]

    def conv_b_step(step, carry):
        pg_cols = pl.ds(pl.multiple_of(step * pg_chunk, pg_chunk), pg_chunk)
        in_cols = pl.ds(pl.multiple_of(5 * d + step * pg_chunk, pg_chunk), pg_chunk)
        pg = _dot(h_buf[...], w_in_ref[:, in_cols])
        ya_cols = pl.ds(pl.multiple_of(step * ya_chunk, ya_chunk), ya_chunk)
        y_a = _dot(lhs_buf[...], w_out_a_ref[:, ya_cols])
        never = jnp.full((SUBLANES, LANES), seq_tile, jnp.int32) < 0
        items = [(b, r) for b in range(CONV_BLOCKS_PER_STEP) for r in range(tm // SUBLANES)]
        prev = None
        for g0 in range(0, len(items), CONV_CHAINS):
            group = items[g0:g0 + CONV_CHAINS]
            acc = [None] * len(group)
            for k in range(K_B):
                for j, (b, r) in enumerate(group):
                    cb = step * CONV_BLOCKS_PER_STEP + b
                    cols = pl.ds(pl.multiple_of(cb * LANES, LANES), LANES)
                    start = HIST_B - (K_B - 1) + k + r * SUBLANES
                    term = conv_b_ref[k, :, cols] * u_buf[cb, start:start + SUBLANES, :]
                    if k > 0:
                        acc[j] = acc[j] + term
                    elif prev is None:
                        acc[j] = term
                    else:
                        acc[j] = jnp.where(never, prev[j % len(prev)], term)
            for j, (b, r) in enumerate(group):
                cb = step * CONV_BLOCKS_PER_STEP + b
                cols = pl.ds(pl.multiple_of(cb * LANES, LANES), LANES)
                v_buf[r * SUBLANES:(r + 1) * SUBLANES, cols] = acc[j]
            prev = acc
        for b in range(CONV_BLOCKS_PER_STEP):
            cb = step * CONV_BLOCKS_PER_STEP + b
            u_buf[cb, 0:HIST_B, :] = u_buf[cb, tm:tm + HIST_B, :]
            cx_buf[cb, HIST_A:HIST_A + tm, :] = y_a[:, b * LANES:(b + 1) * LANES]
        pg_buf[:, pg_cols] = pg
        return carry

    lax.fori_loop(0, n_steps, conv_b_step, 0)

    y_a = jnp.concatenate([cx_buf[cb, HIST_A:HIST_A + tm, :] for cb in range(ncb)], axis=-1)
    merged = gate(0) * y_a

    c_in = pg_group(0)
    for cb in range(ncb):
        c_buf[cb, HIST_C:HIST_C + tm, :] = c_in[:, cb * LANES:(cb + 1) * LANES]
    t_pos = seq_tile * tm + lax.broadcasted_iota(jnp.int32, (tm, 1), 0)
    counts = {w: jnp.minimum(t_pos + 1, w).astype(jnp.float32) for w in POOL_WINDOWS}
    for cb in range(ncb):
        cols = slice(cb * LANES, (cb + 1) * LANES)
        w = POOL_WINDOWS[(cb * LANES) // gc]
        win = None
        for j in range(w):
            term = c_buf[cb, HIST_C - j:HIST_C - j + tm, :]
            win = term if win is None else win + term
        lhs_buf[:, cols] = (win / counts[w] - c_buf[cb, HIST_C:HIST_C + tm, :]).astype(jnp.bfloat16)
        c_buf[cb, 0:HIST_C, :] = c_buf[cb, tm:tm + HIST_C, :]
    y_c = jnp.concatenate(
        [_dot(lhs_buf[:, g * gc:(g + 1) * gc], w_pool_ref[g]) for g in range(len(POOL_WINDOWS))],
        axis=-1) * vec(V_POOL_SCALE)
    merged = merged + gate(2) * y_c

    v = v_buf[...] + vec(V_CONV_B_BIAS)
    mu = jnp.mean(v, axis=-1, keepdims=True)
    vc = v - mu
    var = jnp.mean(vc * vc, axis=-1, keepdims=True)
    ln = vc * lax.rsqrt(var + EPS) * vec(V_LN_G) + vec(V_LN_B)
    s = (ln * jax.nn.sigmoid(ln)).astype(jnp.bfloat16)
    merged = merged + gate(1) * (_dot(s, w_out_b_ref[...]) + vec(V_B_OUT_B))

    o_ref[...] = x_ref[...] + _dot(merged.astype(jnp.bfloat16), w_o_ref[...])


def _mlp_kernel(x_ref, g_ref, w1_ref, w2_ref, gf_ref, o_ref, *, final_norm):
    x = x_ref[...]
    h = _rmsnorm(x, g_ref[...]).astype(jnp.bfloat16)
    d_ff = w1_ref.shape[1]
    chunk = d_ff // MLP_HIDDEN_CHUNKS
    y = x
    for j in range(MLP_HIDDEN_CHUNKS):
        a = _dot(h, w1_ref[:, j * chunk:(j + 1) * chunk])
        a = jnp.square(jnp.maximum(a, 0.0)).astype(jnp.bfloat16)
        y = y + _dot(a, w2_ref[j * chunk // 2:(j + 1) * chunk // 2, :])
    if final_norm:
        y = _rmsnorm(y, gf_ref[...])
    o_ref[...] = y


def _layer_resident(stacked, layer):
    block = (None,) + stacked.shape[1:]
    index = (layer,) + (0,) * (stacked.ndim - 1)
    return pl.BlockSpec(block, lambda i: index, pipeline_mode=pl.Buffered(1))


def _mixer_call(x2, operands, *, layer, seq, tm):
    n, d = x2.shape
    ncb = d // LANES
    tile = pl.BlockSpec((tm, d), lambda i: (i, 0))
    return pl.pallas_call(
        functools.partial(_mixer_kernel, tiles_per_seq=seq // tm),
        grid=(n // tm,),
        in_specs=[tile] + [_layer_resident(a, layer) for a in operands],
        out_specs=tile,
        out_shape=jax.ShapeDtypeStruct((n, d), jnp.float32),
        scratch_shapes=[
            pltpu.VMEM((tm, d), jnp.bfloat16),
            pltpu.VMEM((ncb, HIST_A + tm, LANES), jnp.float32),
            pltpu.VMEM((ncb, HIST_B + tm, LANES), jnp.float32),
            pltpu.VMEM((ncb, HIST_C + tm, LANES), jnp.float32),
            pltpu.VMEM((tm, d), jnp.float32),
            pltpu.VMEM((tm, (1 + N_BRANCH) * d), jnp.float32),
            pltpu.VMEM((tm, d), jnp.bfloat16),
        ],
        compiler_params=pltpu.CompilerParams(
            dimension_semantics=("arbitrary",), vmem_limit_bytes=VMEM_LIMIT_BYTES),
        name="mixer",
    )(x2, *operands)


def _mlp_call(x2, operands, g_final, *, layer, final_norm, tm):
    n, d = x2.shape
    tile = pl.BlockSpec((tm, d), lambda i: (i, 0))
    return pl.pallas_call(
        functools.partial(_mlp_kernel, final_norm=final_norm),
        grid=(n // tm,),
        in_specs=([tile] + [_layer_resident(a, layer) for a in operands]
                  + [_layer_resident(g_final, 0)]),
        out_specs=tile,
        out_shape=jax.ShapeDtypeStruct((n, d), jnp.float32),
        compiler_params=pltpu.CompilerParams(
            dimension_semantics=("arbitrary",), vmem_limit_bytes=VMEM_LIMIT_BYTES),
        name="mlp",
    )(x2, *operands, g_final)


def kernel(x, g_mix, w_in, b_in, conv_a, w_out_a, conv_b, conv_b_bias, ln_b_g, ln_b_b, w_out_b,
           b_out_b, w_pool, pool_scale, w_o, g_mlp, w_mlp1, w_mlp2, g_final):
    bsz, seq, d = x.shape
    depth = w_in.shape[0]
    tm_mixer = 512
    tm_mlp = 1024
    assert seq % tm_mixer == 0 and (bsz * seq) % tm_mlp == 0
    assert d % (LANES * CONV_BLOCKS_PER_STEP) == 0
    assert conv_a.shape[1] == K_A and conv_b.shape[1] == K_B
    n_groups, gc = w_pool.shape[1], w_pool.shape[2]

    vecs = jnp.zeros((depth, N_VEC, d), jnp.float32)
    for row, value in ((V_G_MIX, g_mix), (V_CONV_B_BIAS, conv_b_bias), (V_LN_G, ln_b_g),
                       (V_LN_B, ln_b_b), (V_B_OUT_B, b_out_b), (V_POOL_SCALE, pool_scale)):
        vecs = vecs.at[:, row].set(value)
    conv_b8 = jnp.broadcast_to(conv_b[:, :, None, :], (depth, K_B, SUBLANES, d))
    w_pool_p = _pack_weight(w_pool.reshape(depth * n_groups, gc, gc)).reshape(
        depth, n_groups, gc // 2, gc)
    mixer_operands = (vecs, _pack_weight(w_in), conv_a, _pack_weight(w_out_a), conv_b8,
                      _pack_weight(w_out_b), w_pool_p, _pack_weight(w_o),
                      b_in.reshape(depth, 1, -1))
    mlp_operands = (g_mlp.reshape(depth, 1, d), _pack_weight(w_mlp1), _pack_weight(w_mlp2))
    g_final3 = g_final.reshape(1, 1, d)

    x2 = x.reshape(bsz * seq, d)
    for l in range(depth):
        x2 = _mixer_call(x2, mixer_operands, layer=l, seq=seq, tm=tm_mixer)
        x2 = _mlp_call(x2, mlp_operands, g_final3, layer=l, final_norm=(l == depth - 1), tm=tm_mlp)
    return x2.reshape(bsz, seq, d)
```
